```python
import functools
import jax, jax.numpy as jnp
from jax import lax
import numpy as np

D_MODEL = 2048
BATCH = 4
SEQ = 2048
DEPTH = 2
DEC_BATCH = 128
DEC_SEQ = 4
PAST_LEN = 2048
PAGE_SIZE = 128

HEAD_DIM = 128
ATT_WIDTH = D_MODEL // 2
N_ATT_HEADS = ATT_WIDTH // HEAD_DIM
LRU_WIDTH = D_MODEL - ATT_WIDTH
N_LRU_BLOCKS = 8
LRU_BLOCK = LRU_WIDTH // N_LRU_BLOCKS
MIX_WIDTH = ATT_WIDTH + LRU_WIDTH
CONV_WIDTH = 4
LRU_C = 8.0
N_GROUPS = 4
EXPERTS_PER_GROUP = 8
N_EXPERTS = N_GROUPS * EXPERTS_PER_GROUP
TOP_K = 2
D_EXPERT = D_MODEL // 4
Q_BLOCK = 128
EPS = 1e-6
FORGET_BIAS_INIT = 2.0
IN_COLS = 3 * ATT_WIDTH + N_ATT_HEADS + 2 * LRU_WIDTH
SPLIT_POINTS = (ATT_WIDTH, 2 * ATT_WIDTH, 3 * ATT_WIDTH, 3 * ATT_WIDTH + N_ATT_HEADS,
                3 * ATT_WIDTH + N_ATT_HEADS + LRU_WIDTH)

kernel_name = "hymba_fox_rglru_hmoe_step"


def rms_norm(x, g):
    xf = x.astype(jnp.float32)
    y = xf * lax.rsqrt(jnp.mean(xf * xf, axis=-1, keepdims=True) + EPS)
    return (y * g.astype(jnp.float32)).astype(x.dtype)


def causal_conv(u, buf, w, b):
    seq = u.shape[1]
    ext = jnp.concatenate([buf, u], axis=1)
    y = b + sum(ext[:, i:i + seq] * w[i] for i in range(CONV_WIDTH))
    return y, ext[:, -(CONV_WIDTH - 1):]


def rg_lru(u, h0, w_gate_a, b_gate_a, w_gate_x, b_gate_x, lru_lambda):
    bsz, seq, _ = u.shape
    ub = u.reshape(bsz, seq, N_LRU_BLOCKS, LRU_BLOCK)
    r = jax.nn.sigmoid(jnp.einsum('bshi,hij->bshj', ub, w_gate_a).reshape(bsz, seq, LRU_WIDTH) + b_gate_a)
    ig = jax.nn.sigmoid(jnp.einsum('bshi,hij->bshj', ub, w_gate_x).reshape(bsz, seq, LRU_WIDTH) + b_gate_x)
    log_a = -LRU_C * r.astype(jnp.float32) * jax.nn.softplus(-lru_lambda.astype(jnp.float32))
    a = jnp.exp(log_a)
    mult = jnp.sqrt(-jnp.expm1(2.0 * log_a))
    bterm = mult * (ig * u).astype(jnp.float32)
    bterm = bterm.at[:, 0].add(a[:, 0] * h0.astype(jnp.float32))

    def combine(c1, c2):
        a1, b1 = c1
        a2, b2 = c2
        return a1 * a2, a2 * b1 + b2

    _, h = lax.associative_scan(combine, (a, bterm), axis=1)
    return h.astype(u.dtype), h[:, -1].astype(u.dtype)


def forgetting_attention_block(q, k, v, c_q, c_k, q_start):
    nq, nk = q.shape[1], k.shape[1]
    s = jnp.einsum('bqhd,bkhd->bhqk', q, k).astype(jnp.float32) * (HEAD_DIM ** -0.5)
    s = s + (jnp.transpose(c_q, (0, 2, 1))[:, :, :, None] - jnp.transpose(c_k, (0, 2, 1))[:, :, None, :])
    q_pos = q_start + jnp.arange(nq)
    k_pos = jnp.arange(nk)
    s = jnp.where(k_pos[None, :] <= q_pos[:, None], s, -jnp.inf)
    p = jax.nn.softmax(s, axis=-1)
    return jnp.einsum('bhqk,bkhd->bqhd', p.astype(v.dtype), v)


def attention_prompt(q, k, v, logf):
    bsz, seq = q.shape[0], q.shape[1]
    c = jnp.cumsum(logf.astype(jnp.float32), axis=1)
    nb = seq // Q_BLOCK
    qb = q.reshape(bsz, nb, Q_BLOCK, N_ATT_HEADS, HEAD_DIM).swapaxes(0, 1)
    cb = c.reshape(bsz, nb, Q_BLOCK, N_ATT_HEADS).swapaxes(0, 1)
    starts = jnp.arange(nb) * Q_BLOCK
    out = lax.map(lambda xs: forgetting_attention_block(xs[0], k, v, xs[1], c, xs[2]), (qb, cb, starts))
    return out.swapaxes(0, 1).reshape(bsz, seq, N_ATT_HEADS, HEAD_DIM)


def attention_sample(ck_pages, cv_pages, clf_pages, page_table, q, k_new, v_new, logf_new):
    db = q.shape[0]
    past = page_table.shape[1] * PAGE_SIZE
    k_past = ck_pages[page_table].reshape(db, past, N_ATT_HEADS, HEAD_DIM)
    v_past = cv_pages[page_table].reshape(db, past, N_ATT_HEADS, HEAD_DIM)
    lf_past = clf_pages[page_table].reshape(db, past, N_ATT_HEADS)
    k = jnp.concatenate([k_past, k_new], axis=1)
    v = jnp.concatenate([v_past, v_new], axis=1)
    c = jnp.cumsum(jnp.concatenate([lf_past.astype(jnp.float32), logf_new.astype(jnp.float32)], axis=1), axis=1)
    return forgetting_attention_block(q, k, v, c[:, past:], c, past)


def hybrid_mixer(x, h0, conv_buf, attend, norm_mix, w_in, b_forget, conv_w, conv_b,
                 w_gate_a, b_gate_a, w_gate_x, b_gate_x, lru_lambda, norm_lru_out, norm_att_out, w_out):
    bsz, seq, _ = x.shape
    xn = rms_norm(x, norm_mix)
    z = xn @ w_in
    q, k, v, f_logit, lru_x, lru_gate = jnp.split(z, list(SPLIT_POINTS), axis=-1)
    q = q.reshape(bsz, seq, N_ATT_HEADS, HEAD_DIM)
    k = k.reshape(bsz, seq, N_ATT_HEADS, HEAD_DIM)
    v = v.reshape(bsz, seq, N_ATT_HEADS, HEAD_DIM)
    logf = jax.nn.log_sigmoid((f_logit + b_forget).astype(jnp.float32))
    att = attend(q, k, v, logf).reshape(bsz, seq, ATT_WIDTH)
    u, new_buf = causal_conv(lru_x, conv_buf, conv_w, conv_b)
    h, h_last = rg_lru(u, h0, w_gate_a, b_gate_a, w_gate_x, b_gate_x, lru_lambda)
    lru = h * jax.nn.gelu(lru_gate)
    mixed = jnp.concatenate([rms_norm(lru, norm_lru_out), rms_norm(att, norm_att_out)], axis=-1)
    y = x + mixed @ w_out
    return y, k, v, logf.astype(x.dtype), h_last, new_buf


def hierarchical_moe(x, norm_ffn, w_router_group, b_router_group, w_router_expert, b_router_expert,
                     w_gate, w_up, w_down):
    shp = x.shape
    xn = rms_norm(x, norm_ffn).reshape(-1, D_MODEL)
    g_prob = jax.nn.softmax((xn @ w_router_group + b_router_group).astype(jnp.float32), axis=-1)
    g_top, g_idx = lax.top_k(g_prob, 1)
    e_logits = (xn @ w_router_expert + b_router_expert).astype(jnp.float32)
    e_logits = e_logits.reshape(-1, N_GROUPS, EXPERTS_PER_GROUP)
    e_logits = jnp.take_along_axis(e_logits, g_idx[:, :, None], axis=1)[:, 0]
    e_top, e_idx = lax.top_k(jax.nn.softmax(e_logits, axis=-1), TOP_K)
    e_w = e_top / jnp.sum(e_top, axis=-1, keepdims=True) * g_top
    expert_id = g_idx * EXPERTS_PER_GROUP + e_idx
    combine = jnp.sum(jax.nn.one_hot(expert_id, N_EXPERTS, dtype=jnp.float32) * e_w[..., None], axis=1)
    hid = jax.nn.silu(jnp.einsum('td,edf->tef', xn, w_gate)) * jnp.einsum('td,edf->tef', xn, w_up)
    hid = hid * combine[..., None].astype(hid.dtype)
    y = jnp.einsum('tef,efd->td', hid, w_down)
    return x + y.reshape(shp)


def setup_inputs(seed: int = 0) -> dict:
    key = jax.random.key(seed)
    ks = jax.random.split(key, 32)
    f32 = jnp.float32
    n_pages = PAST_LEN // PAGE_SIZE
    n_used = DEC_BATCH * n_pages
    n_pool = n_used + n_used // 4

    def nrm(k, shape, scale):
        return jax.random.normal(k, shape, f32) * scale

    x_prompt = nrm(ks[0], (BATCH, SEQ, D_MODEL), 1.0)
    x_sample = nrm(ks[1], (DEC_BATCH, DEC_SEQ, D_MODEL), 1.0)
    cache_k = nrm(ks[2], (DEPTH, n_pool, PAGE_SIZE, N_ATT_HEADS, HEAD_DIM), 1.0)
    cache_v = nrm(ks[3], (DEPTH, n_pool, PAGE_SIZE, N_ATT_HEADS, HEAD_DIM), 1.0)
    cache_logf = jax.nn.log_sigmoid(nrm(ks[4], (DEPTH, n_pool, PAGE_SIZE, N_ATT_HEADS), 1.0) + FORGET_BIAS_INIT)
    state_h = nrm(ks[5], (DEPTH, DEC_BATCH, LRU_WIDTH), 0.5)
    state_conv = nrm(ks[6], (DEPTH, DEC_BATCH, CONV_WIDTH - 1, LRU_WIDTH), 1.0)
    page_table = jax.random.permutation(ks[7], n_pool)[:n_used].reshape(DEC_BATCH, n_pages).astype(jnp.int32)

    a_init = jax.random.uniform(ks[8], (DEPTH, LRU_WIDTH), f32, 0.9, 0.999)
    return {
        "x_prompt": x_prompt,
        "x_sample": x_sample,
        "cache_k": cache_k,
        "cache_v": cache_v,
        "cache_logf": cache_logf,
        "state_h": state_h,
        "state_conv": state_conv,
        "page_table": page_table,
        "norm_mix": 1.0 + nrm(ks[9], (DEPTH, D_MODEL), 0.02),
        "w_in": nrm(ks[10], (DEPTH, D_MODEL, IN_COLS), D_MODEL ** -0.5),
        "b_forget": FORGET_BIAS_INIT + nrm(ks[11], (DEPTH, N_ATT_HEADS), 0.1),
        "conv_w": nrm(ks[12], (DEPTH, CONV_WIDTH, LRU_WIDTH), CONV_WIDTH ** -0.5),
        "conv_b": nrm(ks[13], (DEPTH, LRU_WIDTH), 0.01),
        "w_gate_a": nrm(ks[14], (DEPTH, N_LRU_BLOCKS, LRU_BLOCK, LRU_BLOCK), LRU_BLOCK ** -0.5),
        "b_gate_a": nrm(ks[15], (DEPTH, LRU_WIDTH), 0.01),
        "w_gate_x": nrm(ks[16], (DEPTH, N_LRU_BLOCKS, LRU_BLOCK, LRU_BLOCK), LRU_BLOCK ** -0.5),
        "b_gate_x": nrm(ks[17], (DEPTH, LRU_WIDTH), 0.01),
        "lru_lambda": jnp.log(a_init) - jnp.log1p(-a_init),
        "norm_lru_out": 1.0 + nrm(ks[18], (DEPTH, LRU_WIDTH), 0.02),
        "norm_att_out": 1.0 + nrm(ks[19], (DEPTH, ATT_WIDTH), 0.02),
        "w_out": nrm(ks[20], (DEPTH, MIX_WIDTH, D_MODEL), MIX_WIDTH ** -0.5),
        "norm_ffn": 1.0 + nrm(ks[21], (DEPTH, D_MODEL), 0.02),
        "w_router_group": nrm(ks[22], (DEPTH, D_MODEL, N_GROUPS), D_MODEL ** -0.5),
        "b_router_group": nrm(ks[23], (DEPTH, N_GROUPS), 0.01),
        "w_router_expert": nrm(ks[24], (DEPTH, D_MODEL, N_EXPERTS), D_MODEL ** -0.5),
        "b_router_expert": nrm(ks[25], (DEPTH, N_EXPERTS), 0.01),
        "w_gate": nrm(ks[26], (DEPTH, N_EXPERTS, D_MODEL, D_EXPERT), D_MODEL ** -0.5),
        "w_up": nrm(ks[27], (DEPTH, N_EXPERTS, D_MODEL, D_EXPERT), D_MODEL ** -0.5),
        "w_down": nrm(ks[28], (DEPTH, N_EXPERTS, D_EXPERT, D_MODEL), D_EXPERT ** -0.5),
        "norm_final": 1.0 + nrm(ks[29], (D_MODEL,), 0.02),
    }


def reference(x_prompt, x_sample, cache_k, cache_v, cache_logf, state_h, state_conv, page_table,
              norm_mix, w_in, b_forget, conv_w, conv_b, w_gate_a, b_gate_a, w_gate_x, b_gate_x,
              lru_lambda, norm_lru_out, norm_att_out, w_out, norm_ffn, w_router_group, b_router_group,
              w_router_expert, b_router_expert, w_gate, w_up, w_down, norm_final):
    yp, ys = x_prompt, x_sample
    bsz = x_prompt.shape[0]
    kp_l, vp_l, lfp_l, hp_l, cp_l = [], [], [], [], []
    ks_l, vs_l, lfs_l, hs_l, cs_l = [], [], [], [], []
    for l in range(DEPTH):
        mix_params = (norm_mix[l], w_in[l], b_forget[l], conv_w[l], conv_b[l], w_gate_a[l], b_gate_a[l],
                      w_gate_x[l], b_gate_x[l], lru_lambda[l], norm_lru_out[l], norm_att_out[l], w_out[l])
        h0_p = jnp.zeros((bsz, LRU_WIDTH), yp.dtype)
        buf_p = jnp.zeros((bsz, CONV_WIDTH - 1, LRU_WIDTH), yp.dtype)
        yp, kp, vp, lfp, hp, cp = hybrid_mixer(yp, h0_p, buf_p, attention_prompt, *mix_params)
        attend_s = functools.partial(attention_sample, cache_k[l], cache_v[l], cache_logf[l], page_table)
        ys, k_s, v_s, lf_s, h_s, c_s = hybrid_mixer(ys, state_h[l], state_conv[l], attend_s, *mix_params)
        moe_params = (norm_ffn[l], w_router_group[l], b_router_group[l], w_router_expert[l],
                      b_router_expert[l], w_gate[l], w_up[l], w_down[l])
        yp = hierarchical_moe(yp, *moe_params)
        ys = hierarchical_moe(ys, *moe_params)
        kp_l.append(kp); vp_l.append(vp); lfp_l.append(lfp); hp_l.append(hp); cp_l.append(cp)
        ks_l.append(k_s); vs_l.append(v_s); lfs_l.append(lf_s); hs_l.append(h_s); cs_l.append(c_s)
    y_prompt = rms_norm(yp, norm_final)
    y_sample = rms_norm(ys, norm_final)
    new_k_prompt = jnp.stack(kp_l)
    new_v_prompt = jnp.stack(vp_l)
    new_logf_prompt = jnp.stack(lfp_l)
    new_h_prompt = jnp.stack(hp_l)
    new_conv_prompt = jnp.stack(cp_l)
    new_k_sample = jnp.stack(ks_l)
    new_v_sample = jnp.stack(vs_l)
    new_logf_sample = jnp.stack(lfs_l)
    new_h_sample = jnp.stack(hs_l)
    new_conv_sample = jnp.stack(cs_l)
    return (y_prompt, y_sample, new_k_prompt, new_v_prompt, new_logf_prompt, new_h_prompt, new_conv_prompt,
            new_k_sample, new_v_sample, new_logf_sample, new_h_sample, new_conv_sample)
```

```python
import functools

import jax
import jax.numpy as jnp
from jax import lax
from jax.experimental import pallas as pl
from jax.experimental.pallas import tpu as pltpu

F32 = jnp.float32
BF16 = jnp.bfloat16

D_MODEL = 2048
HEAD_DIM = 128
ATT_WIDTH = 1024
N_HEADS = 8
LRU_WIDTH = 1024
N_LRU_BLOCKS = 8
LRU_BLOCK = 128
CONV_WIDTH = 4
LRU_C = 8.0
N_GROUPS = 4
EXPERTS_PER_GROUP = 8
N_EXPERTS = 32
TOP_K = 2
D_EXPERT = 512
PAGE_SIZE = 128
EPS = 1e-6
SCALE = HEAD_DIM ** -0.5
NEG_BIG = -1e30

LANES = 128
SUBLANES = 8
MIB = 1024 * 1024

ROW_TILE = 512
ATT_TILE = 512
LRU_CHUNK = 256
PAGES_PER_STEP = 8
EXPERT_TILE = 256
DISPATCH_TILE = 512
COMBINE_TILE = 256


def _params(semantics, vmem_mib):
    return pltpu.CompilerParams(dimension_semantics=semantics, vmem_limit_bytes=vmem_mib * MIB)


def _sds(shape, dtype):
    return jax.ShapeDtypeStruct(shape, dtype)


def _rms(x, g):
    ms = jnp.mean(x * x, axis=-1, keepdims=True)
    return (x * lax.rsqrt(ms + EPS)) * g


def _sigmoid(x):
    return 1.0 / (1.0 + jnp.exp(-x))


def _softplus(x):
    return jnp.maximum(x, 0.0) + jnp.log1p(jnp.exp(-jnp.abs(x)))


def _gelu_tanh(x):
    return 0.5 * x * (1.0 + jnp.tanh(0.7978845608028654 * (x + 0.044715 * (x * x * x))))


def _in_proj_body(x_ref, g_ref, w_ref, wf_ref, z_ref, f_ref, xn_ref):
    @pl.when(pl.program_id(1) == 0)
    def _():
        xb = _rms(x_ref[...], g_ref[...]).astype(BF16)
        xn_ref[...] = xb
        f_ref[...] = jnp.dot(xb, wf_ref[...], preferred_element_type=F32)

    z_ref[...] = jnp.dot(xn_ref[...], w_ref[...], preferred_element_type=F32)


def _in_proj(x, g, w_main, w_f):
    t, d = x.shape
    n = w_main.shape[1]
    tm, tn = ROW_TILE, 512
    return pl.pallas_call(
        _in_proj_body,
        grid=(t // tm, n // tn),
        in_specs=[
            pl.BlockSpec((tm, d), lambda i, j: (i, 0)),
            pl.BlockSpec((1, d), lambda i, j: (0, 0)),
            pl.BlockSpec((d, tn), lambda i, j: (0, j)),
            pl.BlockSpec((d, LANES), lambda i, j: (0, 0)),
        ],
        out_specs=[
            pl.BlockSpec((tm, tn), lambda i, j: (i, j)),
            pl.BlockSpec((tm, LANES), lambda i, j: (i, 0)),
        ],
        out_shape=[_sds((t, n), F32), _sds((t, LANES), F32)],
        scratch_shapes=[pltpu.VMEM((tm, d), BF16)],
        compiler_params=_params(("parallel", "arbitrary"), 32),
        name="in_proj",
    )(x, g, w_main, w_f)


def _logf_body(f_ref, b_ref, lf_ref, c_ref, ct_ref, carry_ref, *, chunks_per_seq):
    @pl.when(pl.program_id(0) % chunks_per_seq == 0)
    def _():
        carry_ref[...] = jnp.zeros_like(carry_ref)

    x = f_ref[...] + b_ref[...]
    lf = jnp.minimum(x, 0.0) - jnp.log1p(jnp.exp(-jnp.abs(x)))
    lf_ref[...] = lf
    ts = lf.shape[0]
    row = lax.broadcasted_iota(jnp.int32, (ts, ts), 0)
    col = lax.broadcasted_iota(jnp.int32, (ts, ts), 1)
    tri = jnp.where(col <= row, 1.0, 0.0).astype(BF16)
    hi = lf.astype(BF16)
    rem = lf - hi.astype(F32)
    mid = rem.astype(BF16)
    lo = (rem - mid.astype(F32)).astype(BF16)
    cs = (jnp.dot(tri, hi, preferred_element_type=F32)
          + jnp.dot(tri, mid, preferred_element_type=F32)
          + jnp.dot(tri, lo, preferred_element_type=F32))
    cs = cs + carry_ref[...]
    c_ref[...] = cs
    carry_ref[...] = cs[ts - 1:ts, :]
    ct_ref[...] = cs.T[:SUBLANES, :]


def _logf_cumsum(f, b_pad, seq):
    t = f.shape[0]
    ts = ROW_TILE
    return pl.pallas_call(
        functools.partial(_logf_body, chunks_per_seq=seq // ts),
        grid=(t // ts,),
        in_specs=[
            pl.BlockSpec((ts, LANES), lambda i: (i, 0)),
            pl.BlockSpec((1, LANES), lambda i: (0, 0)),
        ],
        out_specs=[
            pl.BlockSpec((ts, LANES), lambda i: (i, 0)),
            pl.BlockSpec((ts, LANES), lambda i: (i, 0)),
            pl.BlockSpec((SUBLANES, ts), lambda i: (0, i)),
        ],
        out_shape=[_sds((t, LANES), F32), _sds((t, LANES), F32), _sds((SUBLANES, t), F32)],
        scratch_shapes=[pltpu.VMEM((1, LANES), F32)],
        compiler_params=_params(("arbitrary",), 32),
        name="logf_cumsum",
    )(f, b_pad)


def _attn_prompt_body(q_ref, k_ref, v_ref, cq_ref, ck_ref, g_ref, o_ref,
                      qs_ref, acc_ref, m_ref, l_ref):
    qi = pl.program_id(1)
    ki = pl.program_id(2)
    tq = q_ref.shape[0]
    tk = k_ref.shape[0]

    @pl.when(ki == 0)
    def _():
        qs_ref[...] = (q_ref[...] * SCALE).astype(BF16)
        acc_ref[...] = jnp.zeros_like(acc_ref)
        m_ref[...] = jnp.full_like(m_ref, NEG_BIG)
        l_ref[...] = jnp.zeros_like(l_ref)

    def step(masked):
        cq = cq_ref[...]
        ck = ck_ref[...]
        if masked:
            row = lax.broadcasted_iota(jnp.int32, (tq, tk), 0)
            col = lax.broadcasted_iota(jnp.int32, (tq, tk), 1)
            keep = col <= row
        for h in range(N_HEADS):
            sl = slice(h * HEAD_DIM, (h + 1) * HEAD_DIM)
            kb = k_ref[:, sl].astype(BF16)
            vb = v_ref[:, sl].astype(BF16)
            s = lax.dot_general(qs_ref[:, sl], kb, (((1,), (1,)), ((), ())),
                                preferred_element_type=F32)
            s = s + (cq[:, h:h + 1] - ck[h:h + 1, :])
            if masked:
                s = jnp.where(keep, s, NEG_BIG)
            m_prev = m_ref[h]
            m_new = jnp.maximum(m_prev, jnp.max(s, axis=-1, keepdims=True))
            alpha = jnp.exp(m_prev - m_new)
            p = jnp.exp(s - m_new)
            l_ref[h] = alpha * l_ref[h] + jnp.sum(p, axis=-1, keepdims=True)
            m_ref[h] = m_new
            acc_ref[:, sl] = alpha * acc_ref[:, sl] + jnp.dot(
                p.astype(BF16), vb, preferred_element_type=F32)

    @pl.when(ki < qi)
    def _():
        step(False)

    @pl.when(ki == qi)
    def _():
        step(True)
        outs = [acc_ref[:, h * HEAD_DIM:(h + 1) * HEAD_DIM] * (1.0 / l_ref[h])
                for h in range(N_HEADS)]
        att = jnp.concatenate(outs, axis=1)
        o_ref[...] = _rms(att, g_ref[...]).astype(BF16)


def _attn_prompt(z, c, ct, g_att, batch, seq):
    tq = ATT_TILE
    nq = seq // tq
    rows = batch * seq
    return pl.pallas_call(
        _attn_prompt_body,
        grid=(batch, nq, nq),
        in_specs=[
            pl.BlockSpec((tq, ATT_WIDTH), lambda b, qi, ki: (b * nq + qi, 0)),
            pl.BlockSpec((tq, ATT_WIDTH), lambda b, qi, ki: (b * nq + jnp.minimum(ki, qi), 1)),
            pl.BlockSpec((tq, ATT_WIDTH), lambda b, qi, ki: (b * nq + jnp.minimum(ki, qi), 2)),
            pl.BlockSpec((tq, LANES), lambda b, qi, ki: (b * nq + qi, 0)),
            pl.BlockSpec((SUBLANES, tq), lambda b, qi, ki: (0, b * nq + jnp.minimum(ki, qi))),
            pl.BlockSpec((1, ATT_WIDTH), lambda b, qi, ki: (0, 0)),
        ],
        out_specs=pl.BlockSpec((tq, ATT_WIDTH), lambda b, qi, ki: (b * nq + qi, 0)),
        out_shape=_sds((rows, ATT_WIDTH), BF16),
        scratch_shapes=[
            pltpu.VMEM((tq, ATT_WIDTH), BF16),
            pltpu.VMEM((tq, ATT_WIDTH), F32),
            pltpu.VMEM((N_HEADS, tq, 1), F32),
            pltpu.VMEM((N_HEADS, tq, 1), F32),
        ],
        compiler_params=_params(("parallel", "parallel", "arbitrary"), 48),
        name="attn_prompt",
    )(z, z, z, c, ct, g_att)


HEAD_ROWS = 16


def _attn_sample_body(pt_ref, q_ref, kn_ref, vn_ref, cq_ref, ck_ref, ckn_ref, g_ref, *rest,
                      pps, dec):
    del pt_ref
    kp = rest[:pps]
    vp = rest[pps:2 * pps]
    o_ref = rest[2 * pps]
    qf_ref, qs_ref, acc_ref, m_ref, l_ref = rest[2 * pps + 1:]
    j = pl.program_id(1)

    @pl.when(j == 0)
    def _():
        q = q_ref[0] * SCALE
        pad = jnp.zeros((HEAD_ROWS - dec, HEAD_DIM), F32)
        qf = jnp.concatenate(
            [jnp.concatenate([q[:, h * HEAD_DIM:(h + 1) * HEAD_DIM], pad], axis=0)
             for h in range(N_HEADS)], axis=0)
        qf_ref[...] = qf
        qs_ref[...] = qf.astype(BF16)
        acc_ref[...] = jnp.zeros_like(acc_ref)
        m_ref[...] = jnp.full_like(m_ref, NEG_BIG)
        l_ref[...] = jnp.zeros_like(l_ref)

    ck = ck_ref[0]
    for h in range(N_HEADS):
        rs = slice(h * HEAD_ROWS, (h + 1) * HEAD_ROWS)
        qh = qs_ref[rs, :]
        parts = [lax.dot_general(qh, kp[i][0, 0, :, h, :].astype(BF16), (((1,), (1,)), ((), ())),
                                 preferred_element_type=F32) for i in range(pps)]
        s = jnp.concatenate(parts, axis=1) + (cq_ref[0, rs, :] - ck[h:h + 1, :])
        m_prev = m_ref[rs, :]
        m_new = jnp.maximum(m_prev, jnp.max(s, axis=-1, keepdims=True))
        alpha = jnp.exp(m_prev - m_new)
        p = jnp.exp(s - m_new)
        l_ref[rs, :] = alpha * l_ref[rs, :] + jnp.sum(p, axis=-1, keepdims=True)
        m_ref[rs, :] = m_new
        pb = p.astype(BF16)
        pv = jnp.dot(pb[:, 0:PAGE_SIZE], vp[0][0, 0, :, h, :].astype(BF16),
                     preferred_element_type=F32)
        for i in range(1, pps):
            pv = pv + jnp.dot(pb[:, i * PAGE_SIZE:(i + 1) * PAGE_SIZE],
                              vp[i][0, 0, :, h, :].astype(BF16), preferred_element_type=F32)
        acc_ref[rs, :] = alpha * acc_ref[rs, :] + pv

    @pl.when(j == pl.num_programs(1) - 1)
    def _():
        kn = kn_ref[0]
        vn = vn_ref[0]
        ckn = ckn_ref[0]
        t_row = lax.broadcasted_iota(jnp.int32, (HEAD_ROWS, 1), 0)
        outs = []
        for h in range(N_HEADS):
            rs = slice(h * HEAD_ROWS, (h + 1) * HEAD_ROWS)
            cs = slice(h * HEAD_DIM, (h + 1) * HEAD_DIM)
            qf = qf_ref[rs, :]
            cqh = cq_ref[0, rs, :]
            sn = []
            for c in range(dec):
                sc = jnp.sum(qf * kn[c:c + 1, cs], axis=-1, keepdims=True)
                sc = sc + (cqh - ckn[h:h + 1, c:c + 1])
                sn.append(jnp.where(t_row >= c, sc, NEG_BIG))
            m_prev = m_ref[rs, :]
            m_new = m_prev
            for c in range(dec):
                m_new = jnp.maximum(m_new, sn[c])
            alpha = jnp.exp(m_prev - m_new)
            l_new = alpha * l_ref[rs, :]
            acc = alpha * acc_ref[rs, :]
            for c in range(dec):
                pc = jnp.exp(sn[c] - m_new)
                l_new = l_new + pc
                acc = acc + pc * vn[c:c + 1, cs]
            outs.append((acc * (1.0 / l_new))[:dec, :])
        att = jnp.concatenate(outs, axis=1)
        o_ref[0] = _rms(att, g_ref[...])


def _attn_sample(layer, page_table, zs, cq_col, ck_t, g_att, cache_k, cache_v):
    db, dec, _ = zs.shape
    n_pages = page_table.shape[1]
    pps = PAGES_PER_STEP
    nj = n_pages // pps
    nrow = N_HEADS * HEAD_ROWS

    def page_spec(i):
        return pl.BlockSpec((1, 1, PAGE_SIZE, N_HEADS, HEAD_DIM),
                            lambda b, j, pt: (layer, pt[b, j * pps + i], 0, 0, 0))

    in_specs = [
        pl.BlockSpec((1, dec, ATT_WIDTH), lambda b, j, pt: (b, 0, 0)),
        pl.BlockSpec((1, dec, ATT_WIDTH), lambda b, j, pt: (b, 0, 1)),
        pl.BlockSpec((1, dec, ATT_WIDTH), lambda b, j, pt: (b, 0, 2)),
        pl.BlockSpec((1, nrow, 1), lambda b, j, pt: (b, 0, 0)),
        pl.BlockSpec((1, N_HEADS, pps * PAGE_SIZE), lambda b, j, pt: (b, 0, j)),
        pl.BlockSpec((1, N_HEADS, LANES), lambda b, j, pt: (b, 0, n_pages)),
        pl.BlockSpec((1, ATT_WIDTH), lambda b, j, pt: (0, 0)),
    ] + [page_spec(i) for i in range(pps)] + [page_spec(i) for i in range(pps)]
    grid_spec = pltpu.PrefetchScalarGridSpec(
        num_scalar_prefetch=1,
        grid=(db, nj),
        in_specs=in_specs,
        out_specs=pl.BlockSpec((1, dec, ATT_WIDTH), lambda b, j, pt: (b, 0, 0)),
        scratch_shapes=[
            pltpu.VMEM((nrow, HEAD_DIM), F32),
            pltpu.VMEM((nrow, HEAD_DIM), BF16),
            pltpu.VMEM((nrow, HEAD_DIM), F32),
            pltpu.VMEM((nrow, 1), F32),
            pltpu.VMEM((nrow, 1), F32),
        ],
    )
    return pl.pallas_call(
        functools.partial(_attn_sample_body, pps=pps, dec=dec),
        grid_spec=grid_spec,
        out_shape=_sds((db, dec, ATT_WIDTH), F32),
        compiler_params=_params(("parallel", "arbitrary"), 48),
        name="attn_sample",
    )(page_table, zs, zs, zs, cq_col, ck_t, ck_t, g_att,
      *([cache_k] * pps), *([cache_v] * pps))


def _lru_coeffs(u, wa_ref, wx_ref, ba, bx, sp):
    ra, rx = [], []
    for k in range(N_LRU_BLOCKS):
        ub = u[:, k * LRU_BLOCK:(k + 1) * LRU_BLOCK].astype(BF16)
        ra.append(jnp.dot(ub, wa_ref[k], preferred_element_type=F32))
        rx.append(jnp.dot(ub, wx_ref[k], preferred_element_type=F32))
    r = _sigmoid(jnp.concatenate(ra, axis=1) + ba)
    ig = _sigmoid(jnp.concatenate(rx, axis=1) + bx)
    log_a = (-LRU_C) * r * sp
    a = jnp.exp(log_a)
    mult = jnp.sqrt(1.0 - jnp.exp(2.0 * log_a))
    return a, mult * (ig * u)


def _lru_prompt_body(x_ref, gate_ref, cw_ref, cb_ref, wa_ref, wx_ref, ba_ref, bx_ref, lam_ref,
                     g_ref, o_ref, hl_ref, nb_ref, xs_ref, a_ref, b_ref, hc_ref):
    c = pl.program_id(1)
    ts, w = x_ref.shape

    @pl.when(c == 0)
    def _():
        xs_ref[0:SUBLANES, :] = jnp.zeros((SUBLANES, w), F32)
        hc_ref[...] = jnp.zeros_like(hc_ref)

    x = x_ref[...]
    xs_ref[SUBLANES:SUBLANES + ts, :] = x
    cw = cw_ref[...]
    u = cb_ref[...] + x * cw[3:4, :]
    for i in range(CONV_WIDTH - 1):
        shift = CONV_WIDTH - 1 - i
        u = u + xs_ref[SUBLANES - shift:SUBLANES - shift + ts, :] * cw[i:i + 1, :]
    xs_ref[0:SUBLANES, :] = x[ts - SUBLANES:ts, :]

    sp = _softplus(-lam_ref[...])
    a, b = _lru_coeffs(u, wa_ref, wx_ref, ba_ref[...], bx_ref[...], sp)

    sub = lax.broadcasted_iota(jnp.int32, (ts, w), 0) % SUBLANES
    for d in (1, 2, 4):
        a_sh = pltpu.roll(a, d, 0)
        b_sh = pltpu.roll(b, d, 0)
        take = sub >= d
        b = jnp.where(take, a * b_sh + b, b)
        a = jnp.where(take, a * a_sh, a)
    a_ref[...] = a
    b_ref[...] = b
    hc = hc_ref[...]
    for gidx in range(ts // SUBLANES):
        rs = slice(gidx * SUBLANES, (gidx + 1) * SUBLANES)
        h8 = a_ref[rs, :] * hc + b_ref[rs, :]
        b_ref[rs, :] = h8
        hc = jnp.broadcast_to(h8[SUBLANES - 1:SUBLANES, :], (SUBLANES, w))
    hc_ref[...] = hc
    h = b_ref[...]
    o_ref[...] = _rms(h * _gelu_tanh(gate_ref[...]), g_ref[...]).astype(BF16)

    @pl.when(c == pl.num_programs(1) - 1)
    def _():
        hl_ref[0] = h[ts - 1:ts, :]
        nb_ref[0] = x[ts - (CONV_WIDTH - 1):ts, :]


def _lru_prompt(z, cw, cb, wa, wx, ba, bx, lam, g, batch, seq):
    ts = LRU_CHUNK
    nc = seq // ts
    w = LRU_WIDTH
    vec = pl.BlockSpec((1, w), lambda b, c: (0, 0))
    blk = pl.BlockSpec((N_LRU_BLOCKS, LRU_BLOCK, LRU_BLOCK), lambda b, c: (0, 0, 0))
    return pl.pallas_call(
        _lru_prompt_body,
        grid=(batch, nc),
        in_specs=[
            pl.BlockSpec((ts, w), lambda b, c: (b * nc + c, 3)),
            pl.BlockSpec((ts, w), lambda b, c: (b * nc + c, 4)),
            pl.BlockSpec((SUBLANES, w), lambda b, c: (0, 0)),
            vec, blk, blk, vec, vec, vec, vec,
        ],
        out_specs=[
            pl.BlockSpec((ts, w), lambda b, c: (b * nc + c, 0)),
            pl.BlockSpec((1, 1, w), lambda b, c: (b, 0, 0)),
            pl.BlockSpec((1, CONV_WIDTH - 1, w), lambda b, c: (b, 0, 0)),
        ],
        out_shape=[_sds((batch * seq, w), BF16), _sds((batch, 1, w), F32),
                   _sds((batch, CONV_WIDTH - 1, w), F32)],
        scratch_shapes=[
            pltpu.VMEM((ts + SUBLANES, w), F32),
            pltpu.VMEM((ts, w), F32),
            pltpu.VMEM((ts, w), F32),
            pltpu.VMEM((SUBLANES, w), F32),
        ],
        compiler_params=_params(("parallel", "arbitrary"), 40),
        name="lru_prompt",
    )(z, z, cw, cb, wa, wx, ba, bx, lam, g)


def _lru_sample_body(x_ref, gate_ref, cbuf_ref, h0_ref, cw_ref, cb_ref, wa_ref, wx_ref, ba_ref,
                     bx_ref, lam_ref, g_ref, o_ref, hl_ref, nb_ref):
    dec = x_ref.shape[0]
    nbuf = CONV_WIDTH - 1
    ext = [cbuf_ref[i] for i in range(nbuf)] + [x_ref[t] for t in range(dec)]
    cw = cw_ref[...]
    cb = cb_ref[...]
    sp = _softplus(-lam_ref[...])
    h = h0_ref[...]
    for t in range(dec):
        u = cb + ext[t] * cw[0:1, :]
        for i in range(1, CONV_WIDTH):
            u = u + ext[t + i] * cw[i:i + 1, :]
        a, b = _lru_coeffs(u, wa_ref, wx_ref, ba_ref[...], bx_ref[...], sp)
        h = a * h + b
        o_ref[t] = _rms(h * _gelu_tanh(gate_ref[t]), g_ref[...]).astype(BF16)
    hl_ref[...] = h
    for i in range(nbuf):
        nb_ref[i] = ext[dec + i]


def _lru_sample(x_t, gate_t, cbuf_t, h0, cw, cb, wa, wx, ba, bx, lam, g):
    dec, db, w = x_t.shape
    return pl.pallas_call(
        _lru_sample_body,
        out_shape=[_sds((dec, db, w), BF16), _sds((db, w), F32),
                   _sds((CONV_WIDTH - 1, db, w), F32)],
        compiler_params=pltpu.CompilerParams(vmem_limit_bytes=40 * MIB),
        name="lru_sample",
    )(x_t, gate_t, cbuf_t, h0, cw, cb, wa, wx, ba, bx, lam, g)


def _out_proj_body(x_ref, a_ref, b_ref, wa_ref, wb_ref, y_ref):
    y_ref[...] = (x_ref[...]
                  + jnp.dot(a_ref[...], wa_ref[...], preferred_element_type=F32)
                  + jnp.dot(b_ref[...], wb_ref[...], preferred_element_type=F32))


def _out_proj(x, a, b, w_a, w_b):
    t, d = x.shape
    k = a.shape[1]
    tm, tn = ROW_TILE, 512
    return pl.pallas_call(
        _out_proj_body,
        grid=(t // tm, d // tn),
        in_specs=[
            pl.BlockSpec((tm, tn), lambda i, j: (i, j)),
            pl.BlockSpec((tm, k), lambda i, j: (i, 0)),
            pl.BlockSpec((tm, k), lambda i, j: (i, 0)),
            pl.BlockSpec((k, tn), lambda i, j: (0, j)),
            pl.BlockSpec((k, tn), lambda i, j: (0, j)),
        ],
        out_specs=pl.BlockSpec((tm, tn), lambda i, j: (i, j)),
        out_shape=_sds((t, d), F32),
        compiler_params=_params(("parallel", "parallel"), 32),
        name="out_proj",
    )(x, a, b, w_a, w_b)


def _router_body(y_ref, g_ref, wr_ref, br_ref, xn_ref, eid_ref, ew_ref):
    xn = _rms(y_ref[...], g_ref[...])
    xn_ref[...] = xn
    logits = jnp.dot(xn, wr_ref[...], precision=lax.Precision.HIGHEST,
                     preferred_element_type=F32) + br_ref[...]
    tm = logits.shape[0]
    lane = lax.broadcasted_iota(jnp.int32, (tm, LANES), 1)
    is_group = lane < N_GROUPS
    gl = jnp.where(is_group, logits, NEG_BIG)
    gmax = jnp.max(gl, axis=-1, keepdims=True)
    gsum = jnp.sum(jnp.exp(gl - gmax), axis=-1, keepdims=True)
    g_top = 1.0 / gsum
    g_idx = jnp.min(jnp.where(gl == gmax, lane, LANES), axis=-1, keepdims=True)

    lane_group = jnp.where((lane >= N_GROUPS) & (lane < N_GROUPS + N_EXPERTS),
                           (lane - N_GROUPS) // EXPERTS_PER_GROUP, -1)
    active = lane_group == g_idx
    el = jnp.where(active, logits, NEG_BIG)
    emax = jnp.max(el, axis=-1, keepdims=True)
    ee = jnp.exp(el - emax)
    ep = ee / jnp.sum(ee, axis=-1, keepdims=True)
    ep = jnp.where(active, ep, -1.0)
    p1 = jnp.max(ep, axis=-1, keepdims=True)
    i1 = jnp.min(jnp.where(ep == p1, lane, LANES), axis=-1, keepdims=True)
    ep2 = jnp.where(lane == i1, -1.0, ep)
    p2 = jnp.max(ep2, axis=-1, keepdims=True)
    i2 = jnp.min(jnp.where(ep2 == p2, lane, LANES), axis=-1, keepdims=True)
    den = p1 + p2
    w1 = p1 / den * g_top
    w2 = p2 / den * g_top
    eid_ref[...] = jnp.where(lane == 0, i1 - N_GROUPS, jnp.where(lane == 1, i2 - N_GROUPS, 0))
    ew_ref[...] = jnp.where(lane == 0, w1, jnp.where(lane == 1, w2, 0.0))


def _router(y, g, w_r, b_r):
    t, d = y.shape
    tm = ROW_TILE
    return pl.pallas_call(
        _router_body,
        grid=(t // tm,),
        in_specs=[
            pl.BlockSpec((tm, d), lambda i: (i, 0)),
            pl.BlockSpec((1, d), lambda i: (0, 0)),
            pl.BlockSpec((d, LANES), lambda i: (0, 0)),
            pl.BlockSpec((1, LANES), lambda i: (0, 0)),
        ],
        out_specs=[
            pl.BlockSpec((tm, d), lambda i: (i, 0)),
            pl.BlockSpec((tm, LANES), lambda i: (i, 0)),
            pl.BlockSpec((tm, LANES), lambda i: (i, 0)),
        ],
        out_shape=[_sds((t, d), F32), _sds((t, LANES), jnp.int32), _sds((t, LANES), F32)],
        compiler_params=_params(("parallel",), 40),
        name="router",
    )(y, g, w_r, b_r)


def _row_copy(src, src_row, dst, dst_row, sem):
    return pltpu.make_async_copy(src.at[pl.ds(src_row, 1)], dst.at[pl.ds(dst_row, 1)], sem)


def _dispatch_body(slot_ref, xn_hbm, xs_in, xs_out, sem, *, tt):
    del xs_in
    base = pl.program_id(0) * tt

    def issue(r, carry):
        for k in range(TOP_K):
            _row_copy(xn_hbm, base + r, xs_out, slot_ref[0, 0, TOP_K * r + k], sem).start()
        return carry

    lax.fori_loop(0, tt, issue, 0)

    def drain(r, carry):
        for k in range(TOP_K):
            _row_copy(xn_hbm, 0, xs_out, 0, sem).wait()
        return carry

    lax.fori_loop(0, tt, drain, 0)


def _dispatch(slots3, xn, xs_init):
    t, d = xn.shape
    tt = DISPATCH_TILE
    return pl.pallas_call(
        functools.partial(_dispatch_body, tt=tt),
        grid=(t // tt,),
        in_specs=[
            pl.BlockSpec((1, 1, TOP_K * tt), lambda i: (i, 0, 0), memory_space=pltpu.SMEM),
            pl.BlockSpec(memory_space=pl.ANY),
            pl.BlockSpec(memory_space=pl.ANY),
        ],
        out_specs=pl.BlockSpec(memory_space=pl.ANY),
        out_shape=_sds(xs_init.shape, F32),
        scratch_shapes=[pltpu.SemaphoreType.DMA(())],
        input_output_aliases={2: 0},
        compiler_params=_params(("arbitrary",), 16),
        name="moe_dispatch",
    )(slots3, xn, xs_init)


def _expert_body(te_ref, na_ref, xs_ref, wg_ref, wu_ref, wd_ref, o_ref):
    del te_ref
    i = pl.program_id(0)

    @pl.when(i < na_ref[0])
    def _():
        x = xs_ref[...].astype(BF16)
        gate = jnp.dot(x, wg_ref[0, 0].astype(BF16), preferred_element_type=F32)
        up = jnp.dot(x, wu_ref[0, 0].astype(BF16), preferred_element_type=F32)
        hid = (gate * _sigmoid(gate)) * up
        o_ref[...] = jnp.dot(hid.astype(BF16), wd_ref[0, 0].astype(BF16),
                             preferred_element_type=F32)

    @pl.when(i >= na_ref[0])
    def _():
        o_ref[...] = jnp.zeros_like(o_ref)


def _experts(layer, tile_expert, n_active, xs, w_gate, w_up, w_down):
    p, d = xs.shape
    tm = EXPERT_TILE
    grid_spec = pltpu.PrefetchScalarGridSpec(
        num_scalar_prefetch=2,
        grid=(p // tm,),
        in_specs=[
            pl.BlockSpec((tm, d), lambda i, te, na: (i, 0)),
            pl.BlockSpec((1, 1, d, D_EXPERT), lambda i, te, na: (layer, te[i], 0, 0)),
            pl.BlockSpec((1, 1, d, D_EXPERT), lambda i, te, na: (layer, te[i], 0, 0)),
            pl.BlockSpec((1, 1, D_EXPERT, d), lambda i, te, na: (layer, te[i], 0, 0)),
        ],
        out_specs=pl.BlockSpec((tm, d), lambda i, te, na: (i, 0)),
    )
    return pl.pallas_call(
        _expert_body,
        grid_spec=grid_spec,
        out_shape=_sds((p, d), F32),
        compiler_params=_params(("arbitrary",), 56),
        name="moe_experts",
    )(tile_expert, n_active, xs, w_gate, w_up, w_down)


def _combine_body(slot_ref, y_ref, w_ref, o_hbm, gf_ref, out_ref, buf, sem, *, tt, final_norm):
    def issue(r, carry):
        for k in range(TOP_K):
            pltpu.make_async_copy(o_hbm.at[pl.ds(slot_ref[0, 0, TOP_K * r + k], 1)],
                                  buf.at[k, pl.ds(r, 1)], sem).start()
        return carry

    lax.fori_loop(0, tt, issue, 0)

    def drain(r, carry):
        for k in range(TOP_K):
            pltpu.make_async_copy(o_hbm.at[pl.ds(0, 1)], buf.at[k, pl.ds(0, 1)], sem).wait()
        return carry

    lax.fori_loop(0, tt, drain, 0)
    w = w_ref[...]
    out = y_ref[...] + w[:, 0:1] * buf[0] + w[:, 1:2] * buf[1]
    if final_norm:
        out = _rms(out, gf_ref[...])
    out_ref[...] = out


def _combine(slots3, y, ew, o, g_final, final_norm):
    t, d = y.shape
    tt = COMBINE_TILE
    return pl.pallas_call(
        functools.partial(_combine_body, tt=tt, final_norm=final_norm),
        grid=(t // tt,),
        in_specs=[
            pl.BlockSpec((1, 1, TOP_K * tt), lambda i: (i, 0, 0), memory_space=pltpu.SMEM),
            pl.BlockSpec((tt, d), lambda i: (i, 0)),
            pl.BlockSpec((tt, LANES), lambda i: (i, 0)),
            pl.BlockSpec(memory_space=pl.ANY),
            pl.BlockSpec((1, d), lambda i: (0, 0)),
        ],
        out_specs=pl.BlockSpec((tt, d), lambda i: (i, 0)),
        out_shape=_sds((t, d), F32),
        scratch_shapes=[pltpu.VMEM((TOP_K, tt, d), F32), pltpu.SemaphoreType.DMA(())],
        compiler_params=_params(("arbitrary",), 32),
        name="moe_combine",
    )(slots3, y, ew, o, g_final)


def _moe_plan(eid2, n_tiles):
    a = eid2.shape[0] * TOP_K
    e = eid2.reshape(a)
    onehot = (e[:, None] == jnp.arange(N_EXPERTS, dtype=jnp.int32)[None, :]).astype(jnp.int32)
    csum = jnp.cumsum(onehot, axis=0)
    counts = csum[-1]
    tiles_e = (counts + EXPERT_TILE - 1) // EXPERT_TILE
    tile_end = jnp.cumsum(tiles_e)
    pstart = (tile_end - tiles_e) * EXPERT_TILE
    slot = jnp.sum(onehot * (pstart[None, :] + csum - onehot), axis=1)
    n_active = tile_end[-1]
    tile_ids = jnp.arange(n_tiles, dtype=jnp.int32)
    te = jnp.sum((tile_ids[:, None] >= tile_end[None, :]).astype(jnp.int32), axis=1)
    te = jnp.minimum(te, N_EXPERTS - 1)
    te_last = te[jnp.maximum(n_active - 1, 0)]
    te = jnp.where(tile_ids < n_active, te, te_last)
    return slot.astype(jnp.int32), te.astype(jnp.int32), n_active.reshape(1).astype(jnp.int32)


def _moe(layer, y, g_ffn, w_r, b_r, w_gate, w_up, w_down, g_final, final_norm):
    t, d = y.shape
    n_tiles = (t * TOP_K) // EXPERT_TILE + N_EXPERTS
    xn, eid, ew = _router(y, g_ffn, w_r, b_r)
    slot, tile_expert, n_active = _moe_plan(eid[:, :TOP_K], n_tiles)
    xs0 = jnp.zeros((n_tiles * EXPERT_TILE, d), F32)
    xs = _dispatch(slot.reshape(t // DISPATCH_TILE, 1, TOP_K * DISPATCH_TILE), xn, xs0)
    o = _experts(layer, tile_expert, n_active, xs, w_gate, w_up, w_down)
    return _combine(slot.reshape(t // COMBINE_TILE, 1, TOP_K * COMBINE_TILE), y, ew, o,
                    g_final, final_norm)


def _pad_lanes(v, width=LANES):
    return jnp.pad(v, [(0, 0)] * (v.ndim - 1) + [(0, width - v.shape[-1])])


def kernel(x_prompt, x_sample, cache_k, cache_v, cache_logf, state_h, state_conv, page_table,
           norm_mix, w_in, b_forget, conv_w, conv_b, w_gate_a, b_gate_a, w_gate_x, b_gate_x,
           lru_lambda, norm_lru_out, norm_att_out, w_out, norm_ffn, w_router_group,
           b_router_group, w_router_expert, b_router_expert, w_gate, w_up, w_down, norm_final):
    batch, seq, d = x_prompt.shape
    db, dec, _ = x_sample.shape
    depth = w_in.shape[0]
    tp = batch * seq
    n_pool = cache_k.shape[1]
    past = page_table.shape[1] * PAGE_SIZE
    qkv = 3 * ATT_WIDTH

    x = jnp.concatenate([x_prompt.reshape(tp, d), x_sample.reshape(db * dec, d)], axis=0)
    outs = {k: [] for k in ("kp", "vp", "lfp", "hp", "cp", "ks", "vs", "lfs", "hs", "cs")}

    for l in range(depth):
        w_main = jnp.concatenate([w_in[l][:, :qkv], w_in[l][:, qkv + N_HEADS:]], axis=1).astype(BF16)
        w_f = _pad_lanes(w_in[l][:, qkv:qkv + N_HEADS]).astype(BF16)
        z, f = _in_proj(x, norm_mix[l][None, :], w_main, w_f)
        logf, c, ct = _logf_cumsum(f, _pad_lanes(b_forget[l][None, :]), seq)

        g_att = norm_att_out[l][None, :]
        att_p = _attn_prompt(z, c, ct, g_att, batch, seq)
        cw = jnp.pad(conv_w[l], ((0, SUBLANES - CONV_WIDTH), (0, 0)))
        lru_args = (cw, conv_b[l][None, :], w_gate_a[l].astype(BF16), w_gate_x[l].astype(BF16),
                    b_gate_a[l][None, :], b_gate_x[l][None, :], lru_lambda[l][None, :],
                    norm_lru_out[l][None, :])
        lru_p, h_p, cbuf_p = _lru_prompt(z, *lru_args, batch, seq)

        zs = z[tp:].reshape(db, dec, z.shape[1])
        lf_new = logf[tp:, :N_HEADS].reshape(db, dec, N_HEADS)
        lf_past = cache_logf[l][page_table].reshape(db, past, N_HEADS)
        c_all = jnp.cumsum(jnp.concatenate([lf_past, lf_new], axis=1), axis=1)
        cq_col = jnp.pad(jnp.swapaxes(c_all[:, past:], 1, 2),
                         ((0, 0), (0, 0), (0, HEAD_ROWS - dec))).reshape(db, N_HEADS * HEAD_ROWS, 1)
        ck_t = _pad_lanes(jnp.swapaxes(c_all, 1, 2), past + LANES)
        att_s = _attn_sample(l, page_table, zs, cq_col, ck_t, g_att, cache_k, cache_v)
        x_t = jnp.swapaxes(zs[:, :, qkv:qkv + LRU_WIDTH], 0, 1)
        gate_t = jnp.swapaxes(zs[:, :, qkv + LRU_WIDTH:], 0, 1)
        lru_s_t, h_s, cbuf_s_t = _lru_sample(x_t, gate_t, jnp.swapaxes(state_conv[l], 0, 1),
                                             state_h[l], *lru_args)

        lru_all = jnp.concatenate([lru_p, jnp.swapaxes(lru_s_t, 0, 1).reshape(db * dec, LRU_WIDTH)])
        att_all = jnp.concatenate([att_p, att_s.reshape(db * dec, ATT_WIDTH).astype(BF16)])
        w_o = w_out[l].astype(BF16)
        y = _out_proj(x, lru_all, att_all, w_o[:LRU_WIDTH], w_o[LRU_WIDTH:])

        w_r = _pad_lanes(jnp.concatenate([w_router_group[l], w_router_expert[l]], axis=1))
        b_r = _pad_lanes(jnp.concatenate([b_router_group[l], b_router_expert[l]])[None, :])
        x = _moe(l, y, norm_ffn[l][None, :], w_r, b_r, w_gate, w_up, w_down,
                 norm_final[None, :], final_norm=(l == depth - 1))

        k_all = z[:, ATT_WIDTH:2 * ATT_WIDTH]
        v_all = z[:, 2 * ATT_WIDTH:qkv]
        outs["kp"].append(k_all[:tp].reshape(batch, seq, N_HEADS, HEAD_DIM))
        outs["vp"].append(v_all[:tp].reshape(batch, seq, N_HEADS, HEAD_DIM))
        outs["lfp"].append(logf[:tp, :N_HEADS].reshape(batch, seq, N_HEADS))
        outs["hp"].append(h_p.reshape(batch, LRU_WIDTH))
        outs["cp"].append(cbuf_p)
        outs["ks"].append(k_all[tp:].reshape(db, dec, N_HEADS, HEAD_DIM))
        outs["vs"].append(v_all[tp:].reshape(db, dec, N_HEADS, HEAD_DIM))
        outs["lfs"].append(lf_new)
        outs["hs"].append(h_s)
        outs["cs"].append(jnp.swapaxes(cbuf_s_t, 0, 1))

    y_prompt = x[:tp].reshape(batch, seq, d)
    y_sample = x[tp:].reshape(db, dec, d)
    st = {k: jnp.stack(v) for k, v in outs.items()}
    return (y_prompt, y_sample, st["kp"], st["vp"], st["lfp"], st["hp"], st["cp"],
            st["ks"], st["vs"], st["lfs"], st["hs"], st["cs"])
```

```python
import functools

import jax
import jax.numpy as jnp
from jax import lax
from jax.experimental import pallas as pl
from jax.experimental.pallas import tpu as pltpu

F32 = jnp.float32
BF16 = jnp.bfloat16

D_MODEL = 2048
HEAD_DIM = 128
ATT_WIDTH = 1024
N_HEADS = 8
LRU_WIDTH = 1024
N_LRU_BLOCKS = 8
LRU_BLOCK = 128
CONV_WIDTH = 4
LRU_C = 8.0
N_GROUPS = 4
EXPERTS_PER_GROUP = 8
N_EXPERTS = 32
TOP_K = 2
D_EXPERT = 512
PAGE_SIZE = 128
EPS = 1e-6
SCALE = HEAD_DIM ** -0.5
NEG_BIG = -1e30

LANES = 128
SUBLANES = 8
MIB = 1024 * 1024

ROW_TILE = 512
ATT_TILE = 512
LRU_CHUNK = 256
PAGES_PER_STEP = 8
EXPERT_TILE = 256
DISPATCH_TILE = 512
COMBINE_TILE = 256


def _params(semantics, vmem_mib):
    return pltpu.CompilerParams(dimension_semantics=semantics, vmem_limit_bytes=vmem_mib * MIB)


def _sds(shape, dtype):
    return jax.ShapeDtypeStruct(shape, dtype)


def _rms(x, g):
    ms = jnp.mean(x * x, axis=-1, keepdims=True)
    return (x * lax.rsqrt(ms + EPS)) * g


def _sigmoid(x):
    return 1.0 / (1.0 + jnp.exp(-x))


def _softplus(x):
    return jnp.maximum(x, 0.0) + jnp.log1p(jnp.exp(-jnp.abs(x)))


def _gelu_tanh(x):
    return 0.5 * x * (1.0 + jnp.tanh(0.7978845608028654 * (x + 0.044715 * (x * x * x))))


def _in_proj_body(x_ref, g_ref, w_ref, wf_ref, z_ref, f_ref, xn_ref):
    @pl.when(pl.program_id(1) == 0)
    def _():
        xb = _rms(x_ref[...], g_ref[...]).astype(BF16)
        xn_ref[...] = xb
        f_ref[...] = jnp.dot(xb, wf_ref[...], preferred_element_type=F32)

    z_ref[...] = jnp.dot(xn_ref[...], w_ref[...], preferred_element_type=F32)


def _in_proj(x, g, w_main, w_f):
    t, d = x.shape
    n = w_main.shape[1]
    tm, tn = ROW_TILE, 512
    return pl.pallas_call(
        _in_proj_body,
        grid=(t // tm, n // tn),
        in_specs=[
            pl.BlockSpec((tm, d), lambda i, j: (i, 0)),
            pl.BlockSpec((1, d), lambda i, j: (0, 0)),
            pl.BlockSpec((d, tn), lambda i, j: (0, j)),
            pl.BlockSpec((d, LANES), lambda i, j: (0, 0)),
        ],
        out_specs=[
            pl.BlockSpec((tm, tn), lambda i, j: (i, j)),
            pl.BlockSpec((tm, LANES), lambda i, j: (i, 0)),
        ],
        out_shape=[_sds((t, n), F32), _sds((t, LANES), F32)],
        scratch_shapes=[pltpu.VMEM((tm, d), BF16)],
        compiler_params=_params(("parallel", "arbitrary"), 32),
        name="in_proj",
    )(x, g, w_main, w_f)


def _logf_body(f_ref, b_ref, lf_ref, c_ref, ct_ref, carry_ref, *, chunks_per_seq):
    @pl.when(pl.program_id(0) % chunks_per_seq == 0)
    def _():
        carry_ref[...] = jnp.zeros_like(carry_ref)

    x = f_ref[...] + b_ref[...]
    lf = jnp.minimum(x, 0.0) - jnp.log1p(jnp.exp(-jnp.abs(x)))
    lf_ref[...] = lf
    ts = lf.shape[0]
    row = lax.broadcasted_iota(jnp.int32, (ts, ts), 0)
    col = lax.broadcasted_iota(jnp.int32, (ts, ts), 1)
    tri = jnp.where(col <= row, 1.0, 0.0).astype(BF16)
    hi = lf.astype(BF16)
    rem = lf - hi.astype(F32)
    mid = rem.astype(BF16)
    lo = (rem - mid.astype(F32)).astype(BF16)
    cs = (jnp.dot(tri, hi, preferred_element_type=F32)
          + jnp.dot(tri, mid, preferred_element_type=F32)
          + jnp.dot(tri, lo, preferred_element_type=F32))
    cs = cs + carry_ref[...]
    c_ref[...] = cs
    carry_ref[...] = cs[ts - 1:ts, :]
    ct_ref[...] = cs.T[:SUBLANES, :]


def _logf_cumsum(f, b_pad, seq):
    t = f.shape[0]
    ts = ROW_TILE
    return pl.pallas_call(
        functools.partial(_logf_body, chunks_per_seq=seq // ts),
        grid=(t // ts,),
        in_specs=[
            pl.BlockSpec((ts, LANES), lambda i: (i, 0)),
            pl.BlockSpec((1, LANES), lambda i: (0, 0)),
        ],
        out_specs=[
            pl.BlockSpec((ts, LANES), lambda i: (i, 0)),
            pl.BlockSpec((ts, LANES), lambda i: (i, 0)),
            pl.BlockSpec((SUBLANES, ts), lambda i: (0, i)),
        ],
        out_shape=[_sds((t, LANES), F32), _sds((t, LANES), F32), _sds((SUBLANES, t), F32)],
        scratch_shapes=[pltpu.VMEM((1, LANES), F32)],
        compiler_params=_params(("arbitrary",), 32),
        name="logf_cumsum",
    )(f, b_pad)


def _attn_prompt_body(q_ref, k_ref, v_ref, cq_ref, ck_ref, g_ref, o_ref,
                      qs_ref, acc_ref, m_ref, l_ref, cqb_ref):
    qi = pl.program_id(1)
    ki = pl.program_id(2)
    tq = q_ref.shape[0]
    tk = k_ref.shape[0]
    reps = tk // LANES

    @pl.when(ki == 0)
    def _():
        qs_ref[...] = (q_ref[...] * SCALE).astype(BF16)
        acc_ref[...] = jnp.zeros_like(acc_ref)
        m_ref[...] = jnp.full_like(m_ref, NEG_BIG)
        l_ref[...] = jnp.zeros_like(l_ref)
        cq = cq_ref[...]
        for h in range(N_HEADS):
            cqb_ref[h] = jnp.broadcast_to(cq[:, h:h + 1], (tq, LANES))

    def widen(x):
        return jnp.concatenate([x] * reps, axis=1)

    def step(masked):
        ck = ck_ref[...]
        if masked:
            row = lax.broadcasted_iota(jnp.int32, (tq, tk), 0)
            col = lax.broadcasted_iota(jnp.int32, (tq, tk), 1)
            keep = col <= row
        for h in range(N_HEADS):
            sl = slice(h * HEAD_DIM, (h + 1) * HEAD_DIM)
            kb = k_ref[:, sl].astype(BF16)
            vb = v_ref[:, sl].astype(BF16)
            s = lax.dot_general(qs_ref[:, sl], kb, (((1,), (1,)), ((), ())),
                                preferred_element_type=F32)
            s = s + (widen(cqb_ref[h]) - ck[h:h + 1, :])
            if masked:
                s = jnp.where(keep, s, NEG_BIG)
            m_prev = m_ref[h]
            m_new = jnp.maximum(m_prev, jnp.max(s, axis=-1, keepdims=True))
            alpha = jnp.exp(m_prev - m_new)
            p = jnp.exp(s - widen(m_new))
            l_ref[h] = alpha * l_ref[h] + jnp.sum(p, axis=-1, keepdims=True)
            m_ref[h] = m_new
            acc_ref[:, sl] = alpha * acc_ref[:, sl] + jnp.dot(
                p.astype(BF16), vb, preferred_element_type=F32)

    @pl.when(ki < qi)
    def _():
        step(False)

    @pl.when(ki == qi)
    def _():
        step(True)
        outs = [acc_ref[:, h * HEAD_DIM:(h + 1) * HEAD_DIM] * (1.0 / l_ref[h])
                for h in range(N_HEADS)]
        att = jnp.concatenate(outs, axis=1)
        o_ref[...] = _rms(att, g_ref[...]).astype(BF16)


def _attn_prompt(z, c, ct, g_att, batch, seq):
    tq = ATT_TILE
    nq = seq // tq
    rows = batch * seq
    return pl.pallas_call(
        _attn_prompt_body,
        grid=(batch, nq, nq),
        in_specs=[
            pl.BlockSpec((tq, ATT_WIDTH), lambda b, qi, ki: (b * nq + qi, 0)),
            pl.BlockSpec((tq, ATT_WIDTH), lambda b, qi, ki: (b * nq + jnp.minimum(ki, qi), 1)),
            pl.BlockSpec((tq, ATT_WIDTH), lambda b, qi, ki: (b * nq + jnp.minimum(ki, qi), 2)),
            pl.BlockSpec((tq, LANES), lambda b, qi, ki: (b * nq + qi, 0)),
            pl.BlockSpec((SUBLANES, tq), lambda b, qi, ki: (0, b * nq + jnp.minimum(ki, qi))),
            pl.BlockSpec((1, ATT_WIDTH), lambda b, qi, ki: (0, 0)),
        ],
        out_specs=pl.BlockSpec((tq, ATT_WIDTH), lambda b, qi, ki: (b * nq + qi, 0)),
        out_shape=_sds((rows, ATT_WIDTH), BF16),
        scratch_shapes=[
            pltpu.VMEM((tq, ATT_WIDTH), BF16),
            pltpu.VMEM((tq, ATT_WIDTH), F32),
            pltpu.VMEM((N_HEADS, tq, LANES), F32),
            pltpu.VMEM((N_HEADS, tq, LANES), F32),
            pltpu.VMEM((N_HEADS, tq, LANES), F32),
        ],
        compiler_params=_params(("parallel", "parallel", "arbitrary"), 48),
        name="attn_prompt",
    )(z, z, z, c, ct, g_att)


def _attn_sample_body(pt_ref, q_ref, kn_ref, vn_ref, cq_ref, ck_ref, ckn_ref, g_ref, *rest,
                      pps, dec):
    del pt_ref
    kp = rest[:pps]
    vp = rest[pps:2 * pps]
    o_ref = rest[2 * pps]
    qf_ref, acc_ref, m_ref, l_ref = rest[2 * pps + 1:]
    j = pl.program_id(1)
    nrow = N_HEADS * dec
    keys = PAGE_SIZE * N_HEADS

    @pl.when(j == 0)
    def _():
        q = q_ref[0] * SCALE
        qf_ref[...] = jnp.concatenate(
            [q[:, h * HEAD_DIM:(h + 1) * HEAD_DIM] for h in range(N_HEADS)], axis=0)
        acc_ref[...] = jnp.zeros_like(acc_ref)
        m_ref[...] = jnp.full_like(m_ref, NEG_BIG)
        l_ref[...] = jnp.zeros_like(l_ref)

    qb = qf_ref[...].astype(BF16)
    cq = cq_ref[0]
    parts = [lax.dot_general(qb, kp[i][0, 0].reshape(keys, HEAD_DIM).astype(BF16),
                             (((1,), (1,)), ((), ())), preferred_element_type=F32)
             for i in range(pps)]
    s = jnp.concatenate(parts, axis=1) + (cq - ck_ref[0])
    col_head = lax.broadcasted_iota(jnp.int32, s.shape, 1) % N_HEADS
    row_head = lax.broadcasted_iota(jnp.int32, s.shape, 0) // dec
    s = jnp.where(col_head == row_head, s, NEG_BIG)
    m_prev = m_ref[...]
    m_new = jnp.maximum(m_prev, jnp.max(s, axis=-1, keepdims=True))
    alpha = jnp.exp(m_prev - m_new)
    p = jnp.exp(s - m_new)
    l_ref[...] = alpha * l_ref[...] + jnp.sum(p, axis=-1, keepdims=True)
    m_ref[...] = m_new
    pb = p.astype(BF16)
    pv = jnp.dot(pb[:, 0:keys], vp[0][0, 0].reshape(keys, HEAD_DIM).astype(BF16),
                 preferred_element_type=F32)
    for i in range(1, pps):
        pv = pv + jnp.dot(pb[:, i * keys:(i + 1) * keys],
                          vp[i][0, 0].reshape(keys, HEAD_DIM).astype(BF16),
                          preferred_element_type=F32)
    acc_ref[...] = alpha * acc_ref[...] + pv

    @pl.when(j == pl.num_programs(1) - 1)
    def _():
        kn = kn_ref[0]
        vn = vn_ref[0]
        ckn = ckn_ref[0]
        qf = qf_ref[...]
        t_row = lax.broadcasted_iota(jnp.int32, (nrow, 1), 0) % dec

        def per_row_head(x, c):
            return jnp.concatenate(
                [jnp.broadcast_to(x[c:c + 1, h * HEAD_DIM:(h + 1) * HEAD_DIM], (dec, HEAD_DIM))
                 for h in range(N_HEADS)], axis=0)

        sn = []
        for c in range(dec):
            sc = jnp.sum(qf * per_row_head(kn, c), axis=-1, keepdims=True)
            sc = sc + (cq - ckn[:, c:c + 1])
            sn.append(jnp.where(t_row >= c, sc, NEG_BIG))
        m_prev = m_ref[...]
        m_new = m_prev
        for c in range(dec):
            m_new = jnp.maximum(m_new, sn[c])
        alpha = jnp.exp(m_prev - m_new)
        l_new = alpha * l_ref[...]
        acc = alpha * acc_ref[...]
        for c in range(dec):
            pc = jnp.exp(sn[c] - m_new)
            l_new = l_new + pc
            acc = acc + pc * per_row_head(vn, c)
        out = acc * (1.0 / l_new)
        att = jnp.concatenate([out[h * dec:(h + 1) * dec, :] for h in range(N_HEADS)], axis=1)
        o_ref[0] = _rms(att, g_ref[...])


def _attn_sample(layer, page_table, zs, cq_col, ck_flat, ckn_col, g_att, cache_k, cache_v):
    db, dec, _ = zs.shape
    n_pages = page_table.shape[1]
    pps = PAGES_PER_STEP
    nj = n_pages // pps
    nrow = N_HEADS * dec
    keys = PAGE_SIZE * N_HEADS

    def page_spec(i):
        return pl.BlockSpec((1, 1, PAGE_SIZE, N_HEADS, HEAD_DIM),
                            lambda b, j, pt: (layer, pt[b, j * pps + i], 0, 0, 0))

    in_specs = [
        pl.BlockSpec((1, dec, ATT_WIDTH), lambda b, j, pt: (b, 0, 0)),
        pl.BlockSpec((1, dec, ATT_WIDTH), lambda b, j, pt: (b, 0, 1)),
        pl.BlockSpec((1, dec, ATT_WIDTH), lambda b, j, pt: (b, 0, 2)),
        pl.BlockSpec((1, nrow, 1), lambda b, j, pt: (b, 0, 0)),
        pl.BlockSpec((1, 1, pps * keys), lambda b, j, pt: (b, 0, j)),
        pl.BlockSpec((1, nrow, LANES), lambda b, j, pt: (b, 0, 0)),
        pl.BlockSpec((1, ATT_WIDTH), lambda b, j, pt: (0, 0)),
    ] + [page_spec(i) for i in range(pps)] + [page_spec(i) for i in range(pps)]
    grid_spec = pltpu.PrefetchScalarGridSpec(
        num_scalar_prefetch=1,
        grid=(db, nj),
        in_specs=in_specs,
        out_specs=pl.BlockSpec((1, dec, ATT_WIDTH), lambda b, j, pt: (b, 0, 0)),
        scratch_shapes=[
            pltpu.VMEM((nrow, HEAD_DIM), F32),
            pltpu.VMEM((nrow, HEAD_DIM), F32),
            pltpu.VMEM((nrow, 1), F32),
            pltpu.VMEM((nrow, 1), F32),
        ],
    )
    return pl.pallas_call(
        functools.partial(_attn_sample_body, pps=pps, dec=dec),
        grid_spec=grid_spec,
        out_shape=_sds((db, dec, ATT_WIDTH), F32),
        compiler_params=_params(("parallel", "arbitrary"), 48),
        name="attn_sample",
    )(page_table, zs, zs, zs, cq_col, ck_flat, ckn_col, g_att,
      *([cache_k] * pps), *([cache_v] * pps))


def _lru_coeffs(u, wa_ref, wx_ref, ba, bx, sp):
    ra, rx = [], []
    for k in range(N_LRU_BLOCKS):
        ub = u[:, k * LRU_BLOCK:(k + 1) * LRU_BLOCK].astype(BF16)
        ra.append(jnp.dot(ub, wa_ref[k], preferred_element_type=F32))
        rx.append(jnp.dot(ub, wx_ref[k], preferred_element_type=F32))
    r = _sigmoid(jnp.concatenate(ra, axis=1) + ba)
    ig = _sigmoid(jnp.concatenate(rx, axis=1) + bx)
    log_a = (-LRU_C) * r * sp
    a = jnp.exp(log_a)
    mult = jnp.sqrt(1.0 - jnp.exp(2.0 * log_a))
    return a, mult * (ig * u)


def _lru_prompt_body(x_ref, gate_ref, cw_ref, cb_ref, wa_ref, wx_ref, ba_ref, bx_ref, lam_ref,
                     g_ref, o_ref, hl_ref, nb_ref, xs_ref, a_ref, b_ref, hc_ref):
    c = pl.program_id(1)
    ts, w = x_ref.shape

    @pl.when(c == 0)
    def _():
        xs_ref[0:SUBLANES, :] = jnp.zeros((SUBLANES, w), F32)
        hc_ref[...] = jnp.zeros_like(hc_ref)

    x = x_ref[...]
    xs_ref[SUBLANES:SUBLANES + ts, :] = x
    cw = cw_ref[...]
    u = cb_ref[...] + x * cw[3:4, :]
    for i in range(CONV_WIDTH - 1):
        shift = CONV_WIDTH - 1 - i
        u = u + xs_ref[SUBLANES - shift:SUBLANES - shift + ts, :] * cw[i:i + 1, :]
    xs_ref[0:SUBLANES, :] = x[ts - SUBLANES:ts, :]

    sp = _softplus(-lam_ref[...])
    a, b = _lru_coeffs(u, wa_ref, wx_ref, ba_ref[...], bx_ref[...], sp)

    sub = lax.broadcasted_iota(jnp.int32, (ts, w), 0) % SUBLANES
    for d in (1, 2, 4):
        a_sh = pltpu.roll(a, d, 0)
        b_sh = pltpu.roll(b, d, 0)
        take = sub >= d
        b = jnp.where(take, a * b_sh + b, b)
        a = jnp.where(take, a * a_sh, a)
    a_ref[...] = a
    b_ref[...] = b
    hc = hc_ref[...]
    for gidx in range(ts // SUBLANES):
        rs = slice(gidx * SUBLANES, (gidx + 1) * SUBLANES)
        h8 = a_ref[rs, :] * hc + b_ref[rs, :]
        b_ref[rs, :] = h8
        hc = jnp.broadcast_to(h8[SUBLANES - 1:SUBLANES, :], (SUBLANES, w))
    hc_ref[...] = hc
    h = b_ref[...]
    o_ref[...] = _rms(h * _gelu_tanh(gate_ref[...]), g_ref[...]).astype(BF16)

    @pl.when(c == pl.num_programs(1) - 1)
    def _():
        hl_ref[0] = h[ts - 1:ts, :]
        nb_ref[0] = x[ts - (CONV_WIDTH - 1):ts, :]


def _lru_prompt(z, cw, cb, wa, wx, ba, bx, lam, g, batch, seq):
    ts = LRU_CHUNK
    nc = seq // ts
    w = LRU_WIDTH
    vec = pl.BlockSpec((1, w), lambda b, c: (0, 0))
    blk = pl.BlockSpec((N_LRU_BLOCKS, LRU_BLOCK, LRU_BLOCK), lambda b, c: (0, 0, 0))
    return pl.pallas_call(
        _lru_prompt_body,
        grid=(batch, nc),
        in_specs=[
            pl.BlockSpec((ts, w), lambda b, c: (b * nc + c, 3)),
            pl.BlockSpec((ts, w), lambda b, c: (b * nc + c, 4)),
            pl.BlockSpec((SUBLANES, w), lambda b, c: (0, 0)),
            vec, blk, blk, vec, vec, vec, vec,
        ],
        out_specs=[
            pl.BlockSpec((ts, w), lambda b, c: (b * nc + c, 0)),
            pl.BlockSpec((1, 1, w), lambda b, c: (b, 0, 0)),
            pl.BlockSpec((1, CONV_WIDTH - 1, w), lambda b, c: (b, 0, 0)),
        ],
        out_shape=[_sds((batch * seq, w), BF16), _sds((batch, 1, w), F32),
                   _sds((batch, CONV_WIDTH - 1, w), F32)],
        scratch_shapes=[
            pltpu.VMEM((ts + SUBLANES, w), F32),
            pltpu.VMEM((ts, w), F32),
            pltpu.VMEM((ts, w), F32),
            pltpu.VMEM((SUBLANES, w), F32),
        ],
        compiler_params=_params(("parallel", "arbitrary"), 40),
        name="lru_prompt",
    )(z, z, cw, cb, wa, wx, ba, bx, lam, g)


def _lru_sample_body(x_ref, gate_ref, cbuf_ref, h0_ref, cw_ref, cb_ref, wa_ref, wx_ref, ba_ref,
                     bx_ref, lam_ref, g_ref, o_ref, hl_ref, nb_ref):
    dec = x_ref.shape[0]
    nbuf = CONV_WIDTH - 1
    ext = [cbuf_ref[i] for i in range(nbuf)] + [x_ref[t] for t in range(dec)]
    cw = cw_ref[...]
    cb = cb_ref[...]
    sp = _softplus(-lam_ref[...])
    h = h0_ref[...]
    for t in range(dec):
        u = cb + ext[t] * cw[0:1, :]
        for i in range(1, CONV_WIDTH):
            u = u + ext[t + i] * cw[i:i + 1, :]
        a, b = _lru_coeffs(u, wa_ref, wx_ref, ba_ref[...], bx_ref[...], sp)
        h = a * h + b
        o_ref[t] = _rms(h * _gelu_tanh(gate_ref[t]), g_ref[...]).astype(BF16)
    hl_ref[...] = h
    for i in range(nbuf):
        nb_ref[i] = ext[dec + i]


def _lru_sample(x_t, gate_t, cbuf_t, h0, cw, cb, wa, wx, ba, bx, lam, g):
    dec, db, w = x_t.shape
    return pl.pallas_call(
        _lru_sample_body,
        out_shape=[_sds((dec, db, w), BF16), _sds((db, w), F32),
                   _sds((CONV_WIDTH - 1, db, w), F32)],
        compiler_params=pltpu.CompilerParams(vmem_limit_bytes=40 * MIB),
        name="lru_sample",
    )(x_t, gate_t, cbuf_t, h0, cw, cb, wa, wx, ba, bx, lam, g)


def _out_proj_body(x_ref, a_ref, b_ref, wa_ref, wb_ref, y_ref):
    y_ref[...] = (x_ref[...]
                  + jnp.dot(a_ref[...], wa_ref[...], preferred_element_type=F32)
                  + jnp.dot(b_ref[...], wb_ref[...], preferred_element_type=F32))


def _out_proj(x, a, b, w_a, w_b):
    t, d = x.shape
    k = a.shape[1]
    tm, tn = ROW_TILE, 512
    return pl.pallas_call(
        _out_proj_body,
        grid=(t // tm, d // tn),
        in_specs=[
            pl.BlockSpec((tm, tn), lambda i, j: (i, j)),
            pl.BlockSpec((tm, k), lambda i, j: (i, 0)),
            pl.BlockSpec((tm, k), lambda i, j: (i, 0)),
            pl.BlockSpec((k, tn), lambda i, j: (0, j)),
            pl.BlockSpec((k, tn), lambda i, j: (0, j)),
        ],
        out_specs=pl.BlockSpec((tm, tn), lambda i, j: (i, j)),
        out_shape=_sds((t, d), F32),
        compiler_params=_params(("parallel", "parallel"), 32),
        name="out_proj",
    )(x, a, b, w_a, w_b)


def _router_body(y_ref, g_ref, wr_ref, br_ref, xn_ref, eid_ref, ew_ref):
    xn = _rms(y_ref[...], g_ref[...])
    xn_ref[...] = xn
    logits = jnp.dot(xn, wr_ref[...], precision=lax.Precision.HIGHEST,
                     preferred_element_type=F32) + br_ref[...]
    tm = logits.shape[0]
    lane = lax.broadcasted_iota(jnp.int32, (tm, LANES), 1)
    is_group = lane < N_GROUPS
    gl = jnp.where(is_group, logits, NEG_BIG)
    gmax = jnp.max(gl, axis=-1, keepdims=True)
    gsum = jnp.sum(jnp.exp(gl - gmax), axis=-1, keepdims=True)
    g_top = 1.0 / gsum
    g_idx = jnp.min(jnp.where(gl == gmax, lane, LANES), axis=-1, keepdims=True)

    lane_group = jnp.where((lane >= N_GROUPS) & (lane < N_GROUPS + N_EXPERTS),
                           (lane - N_GROUPS) // EXPERTS_PER_GROUP, -1)
    active = lane_group == g_idx
    el = jnp.where(active, logits, NEG_BIG)
    emax = jnp.max(el, axis=-1, keepdims=True)
    ee = jnp.exp(el - emax)
    ep = ee / jnp.sum(ee, axis=-1, keepdims=True)
    ep = jnp.where(active, ep, -1.0)
    p1 = jnp.max(ep, axis=-1, keepdims=True)
    i1 = jnp.min(jnp.where(ep == p1, lane, LANES), axis=-1, keepdims=True)
    ep2 = jnp.where(lane == i1, -1.0, ep)
    p2 = jnp.max(ep2, axis=-1, keepdims=True)
    i2 = jnp.min(jnp.where(ep2 == p2, lane, LANES), axis=-1, keepdims=True)
    den = p1 + p2
    w1 = p1 / den * g_top
    w2 = p2 / den * g_top
    eid_ref[...] = jnp.where(lane == 0, i1 - N_GROUPS, jnp.where(lane == 1, i2 - N_GROUPS, 0))
    ew_ref[...] = jnp.where(lane == 0, w1, jnp.where(lane == 1, w2, 0.0))


def _router(y, g, w_r, b_r):
    t, d = y.shape
    tm = ROW_TILE
    return pl.pallas_call(
        _router_body,
        grid=(t // tm,),
        in_specs=[
            pl.BlockSpec((tm, d), lambda i: (i, 0)),
            pl.BlockSpec((1, d), lambda i: (0, 0)),
            pl.BlockSpec((d, LANES), lambda i: (0, 0)),
            pl.BlockSpec((1, LANES), lambda i: (0, 0)),
        ],
        out_specs=[
            pl.BlockSpec((tm, d), lambda i: (i, 0)),
            pl.BlockSpec((tm, LANES), lambda i: (i, 0)),
            pl.BlockSpec((tm, LANES), lambda i: (i, 0)),
        ],
        out_shape=[_sds((t, d), F32), _sds((t, LANES), jnp.int32), _sds((t, LANES), F32)],
        compiler_params=_params(("parallel",), 40),
        name="router",
    )(y, g, w_r, b_r)


def _row_copy(src, src_row, dst, dst_row, sem):
    return pltpu.make_async_copy(src.at[pl.ds(src_row, 1)], dst.at[pl.ds(dst_row, 1)], sem)


def _dispatch_body(slot_ref, xn_ref, xs_in, xs_out, sem, *, tt):
    del xs_in

    def issue(r, carry):
        for k in range(TOP_K):
            _row_copy(xn_ref, r, xs_out, slot_ref[0, 0, TOP_K * r + k], sem).start(priority=k)
        return carry

    lax.fori_loop(0, tt, issue, 0, unroll=8)

    def drain(r, carry):
        for k in range(TOP_K):
            _row_copy(xn_ref, 0, xs_out, 0, sem).wait()
        return carry

    lax.fori_loop(0, tt, drain, 0, unroll=8)


def _dispatch(slots3, xn, xs_init):
    t, d = xn.shape
    tt = DISPATCH_TILE
    return pl.pallas_call(
        functools.partial(_dispatch_body, tt=tt),
        grid=(t // tt,),
        in_specs=[
            pl.BlockSpec((1, 1, TOP_K * tt), lambda i: (i, 0, 0), memory_space=pltpu.SMEM),
            pl.BlockSpec((tt, d), lambda i: (i, 0)),
            pl.BlockSpec(memory_space=pl.ANY),
        ],
        out_specs=pl.BlockSpec(memory_space=pl.ANY),
        out_shape=_sds(xs_init.shape, F32),
        scratch_shapes=[pltpu.SemaphoreType.DMA(())],
        input_output_aliases={2: 0},
        compiler_params=_params(("arbitrary",), 16),
        name="moe_dispatch",
    )(slots3, xn, xs_init)


def _expert_body(te_ref, na_ref, xs_ref, wg_ref, wu_ref, wd_ref, o_ref):
    del te_ref
    i = pl.program_id(0)

    @pl.when(i < na_ref[0])
    def _():
        x = xs_ref[...].astype(BF16)
        gate = jnp.dot(x, wg_ref[0, 0].astype(BF16), preferred_element_type=F32)
        up = jnp.dot(x, wu_ref[0, 0].astype(BF16), preferred_element_type=F32)
        hid = (gate * _sigmoid(gate)) * up
        o_ref[...] = jnp.dot(hid.astype(BF16), wd_ref[0, 0].astype(BF16),
                             preferred_element_type=F32)

    @pl.when(i >= na_ref[0])
    def _():
        o_ref[...] = jnp.zeros_like(o_ref)


def _experts(layer, tile_expert, n_active, xs, w_gate, w_up, w_down):
    p, d = xs.shape
    tm = EXPERT_TILE
    grid_spec = pltpu.PrefetchScalarGridSpec(
        num_scalar_prefetch=2,
        grid=(p // tm,),
        in_specs=[
            pl.BlockSpec((tm, d), lambda i, te, na: (i, 0)),
            pl.BlockSpec((1, 1, d, D_EXPERT), lambda i, te, na: (layer, te[i], 0, 0)),
            pl.BlockSpec((1, 1, d, D_EXPERT), lambda i, te, na: (layer, te[i], 0, 0)),
            pl.BlockSpec((1, 1, D_EXPERT, d), lambda i, te, na: (layer, te[i], 0, 0)),
        ],
        out_specs=pl.BlockSpec((tm, d), lambda i, te, na: (i, 0)),
    )
    return pl.pallas_call(
        _expert_body,
        grid_spec=grid_spec,
        out_shape=_sds((p, d), F32),
        compiler_params=_params(("arbitrary",), 56),
        name="moe_experts",
    )(tile_expert, n_active, xs, w_gate, w_up, w_down)


def _combine_body(slot_ref, y_ref, w_ref, o_hbm, gf_ref, out_ref, buf, sem, *, tt, final_norm):
    def issue(r, carry):
        for k in range(TOP_K):
            pltpu.make_async_copy(o_hbm.at[pl.ds(slot_ref[0, 0, TOP_K * r + k], 1)],
                                  buf.at[k, pl.ds(r, 1)], sem).start(priority=k)
        return carry

    lax.fori_loop(0, tt, issue, 0, unroll=8)

    def drain(r, carry):
        for k in range(TOP_K):
            pltpu.make_async_copy(o_hbm.at[pl.ds(0, 1)], buf.at[k, pl.ds(0, 1)], sem).wait()
        return carry

    lax.fori_loop(0, tt, drain, 0, unroll=8)
    w = w_ref[...]
    out = y_ref[...] + w[:, 0:1] * buf[0] + w[:, 1:2] * buf[1]
    if final_norm:
        out = _rms(out, gf_ref[...])
    out_ref[...] = out


def _combine(slots3, y, ew, o, g_final, final_norm):
    t, d = y.shape
    tt = COMBINE_TILE
    return pl.pallas_call(
        functools.partial(_combine_body, tt=tt, final_norm=final_norm),
        grid=(t // tt,),
        in_specs=[
            pl.BlockSpec((1, 1, TOP_K * tt), lambda i: (i, 0, 0), memory_space=pltpu.SMEM),
            pl.BlockSpec((tt, d), lambda i: (i, 0)),
            pl.BlockSpec((tt, LANES), lambda i: (i, 0)),
            pl.BlockSpec(memory_space=pl.ANY),
            pl.BlockSpec((1, d), lambda i: (0, 0)),
        ],
        out_specs=pl.BlockSpec((tt, d), lambda i: (i, 0)),
        out_shape=_sds((t, d), F32),
        scratch_shapes=[pltpu.VMEM((TOP_K, tt, d), F32), pltpu.SemaphoreType.DMA(())],
        compiler_params=_params(("arbitrary",), 32),
        name="moe_combine",
    )(slots3, y, ew, o, g_final)


def _moe_plan(eid2, n_tiles):
    a = eid2.shape[0] * TOP_K
    e = eid2.reshape(a)
    onehot = (e[:, None] == jnp.arange(N_EXPERTS, dtype=jnp.int32)[None, :]).astype(jnp.int32)
    csum = jnp.cumsum(onehot, axis=0)
    counts = csum[-1]
    tiles_e = (counts + EXPERT_TILE - 1) // EXPERT_TILE
    tile_end = jnp.cumsum(tiles_e)
    pstart = (tile_end - tiles_e) * EXPERT_TILE
    slot = jnp.sum(onehot * (pstart[None, :] + csum - onehot), axis=1)
    n_active = tile_end[-1]
    tile_ids = jnp.arange(n_tiles, dtype=jnp.int32)
    te = jnp.sum((tile_ids[:, None] >= tile_end[None, :]).astype(jnp.int32), axis=1)
    te = jnp.minimum(te, N_EXPERTS - 1)
    te_last = te[jnp.maximum(n_active - 1, 0)]
    te = jnp.where(tile_ids < n_active, te, te_last)
    return slot.astype(jnp.int32), te.astype(jnp.int32), n_active.reshape(1).astype(jnp.int32)


def _moe(layer, y, g_ffn, w_r, b_r, w_gate, w_up, w_down, g_final, final_norm):
    t, d = y.shape
    n_tiles = (t * TOP_K) // EXPERT_TILE + N_EXPERTS
    xn, eid, ew = _router(y, g_ffn, w_r, b_r)
    slot, tile_expert, n_active = _moe_plan(eid[:, :TOP_K], n_tiles)
    xs0 = jnp.zeros((n_tiles * EXPERT_TILE, d), F32)
    xs = _dispatch(slot.reshape(t // DISPATCH_TILE, 1, TOP_K * DISPATCH_TILE), xn, xs0)
    o = _experts(layer, tile_expert, n_active, xs, w_gate, w_up, w_down)
    return _combine(slot.reshape(t // COMBINE_TILE, 1, TOP_K * COMBINE_TILE), y, ew, o,
                    g_final, final_norm)


def _pad_lanes(v, width=LANES):
    return jnp.pad(v, [(0, 0)] * (v.ndim - 1) + [(0, width - v.shape[-1])])


def kernel(x_prompt, x_sample, cache_k, cache_v, cache_logf, state_h, state_conv, page_table,
           norm_mix, w_in, b_forget, conv_w, conv_b, w_gate_a, b_gate_a, w_gate_x, b_gate_x,
           lru_lambda, norm_lru_out, norm_att_out, w_out, norm_ffn, w_router_group,
           b_router_group, w_router_expert, b_router_expert, w_gate, w_up, w_down, norm_final):
    batch, seq, d = x_prompt.shape
    db, dec, _ = x_sample.shape
    depth = w_in.shape[0]
    tp = batch * seq
    n_pool = cache_k.shape[1]
    past = page_table.shape[1] * PAGE_SIZE
    qkv = 3 * ATT_WIDTH

    x = jnp.concatenate([x_prompt.reshape(tp, d), x_sample.reshape(db * dec, d)], axis=0)
    outs = {k: [] for k in ("kp", "vp", "lfp", "hp", "cp", "ks", "vs", "lfs", "hs", "cs")}

    for l in range(depth):
        w_main = jnp.concatenate([w_in[l][:, :qkv], w_in[l][:, qkv + N_HEADS:]], axis=1).astype(BF16)
        w_f = _pad_lanes(w_in[l][:, qkv:qkv + N_HEADS]).astype(BF16)
        z, f = _in_proj(x, norm_mix[l][None, :], w_main, w_f)
        logf, c, ct = _logf_cumsum(f, _pad_lanes(b_forget[l][None, :]), seq)

        g_att = norm_att_out[l][None, :]
        att_p = _attn_prompt(z, c, ct, g_att, batch, seq)
        cw = jnp.pad(conv_w[l], ((0, SUBLANES - CONV_WIDTH), (0, 0)))
        lru_args = (cw, conv_b[l][None, :], w_gate_a[l].astype(BF16), w_gate_x[l].astype(BF16),
                    b_gate_a[l][None, :], b_gate_x[l][None, :], lru_lambda[l][None, :],
                    norm_lru_out[l][None, :])
        lru_p, h_p, cbuf_p = _lru_prompt(z, *lru_args, batch, seq)

        zs = z[tp:].reshape(db, dec, z.shape[1])
        lf_new = logf[tp:, :N_HEADS].reshape(db, dec, N_HEADS)
        lf_past = cache_logf[l][page_table].reshape(db, past, N_HEADS)
        c_all = jnp.cumsum(jnp.concatenate([lf_past, lf_new], axis=1), axis=1)
        c_new_t = jnp.swapaxes(c_all[:, past:], 1, 2)
        cq_col = c_new_t.reshape(db, N_HEADS * dec, 1)
        ck_flat = c_all[:, :past].reshape(db, 1, past * N_HEADS)
        ckn_col = _pad_lanes(jnp.broadcast_to(c_new_t[:, :, None, :], (db, N_HEADS, dec, dec))
                             .reshape(db, N_HEADS * dec, dec))
        att_s = _attn_sample(l, page_table, zs, cq_col, ck_flat, ckn_col, g_att,
                             cache_k, cache_v)
        x_t = jnp.swapaxes(zs[:, :, qkv:qkv + LRU_WIDTH], 0, 1)
        gate_t = jnp.swapaxes(zs[:, :, qkv + LRU_WIDTH:], 0, 1)
        lru_s_t, h_s, cbuf_s_t = _lru_sample(x_t, gate_t, jnp.swapaxes(state_conv[l], 0, 1),
                                             state_h[l], *lru_args)

        lru_all = jnp.concatenate([lru_p, jnp.swapaxes(lru_s_t, 0, 1).reshape(db * dec, LRU_WIDTH)])
        att_all = jnp.concatenate([att_p, att_s.reshape(db * dec, ATT_WIDTH).astype(BF16)])
        w_o = w_out[l].astype(BF16)
        y = _out_proj(x, lru_all, att_all, w_o[:LRU_WIDTH], w_o[LRU_WIDTH:])

        w_r = _pad_lanes(jnp.concatenate([w_router_group[l], w_router_expert[l]], axis=1))
        b_r = _pad_lanes(jnp.concatenate([b_router_group[l], b_router_expert[l]])[None, :])
        x = _moe(l, y, norm_ffn[l][None, :], w_r, b_r, w_gate, w_up, w_down,
                 norm_final[None, :], final_norm=(l == depth - 1))

        k_all = z[:, ATT_WIDTH:2 * ATT_WIDTH]
        v_all = z[:, 2 * ATT_WIDTH:qkv]
        outs["kp"].append(k_all[:tp].reshape(batch, seq, N_HEADS, HEAD_DIM))
        outs["vp"].append(v_all[:tp].reshape(batch, seq, N_HEADS, HEAD_DIM))
        outs["lfp"].append(logf[:tp, :N_HEADS].reshape(batch, seq, N_HEADS))
        outs["hp"].append(h_p.reshape(batch, LRU_WIDTH))
        outs["cp"].append(cbuf_p)
        outs["ks"].append(k_all[tp:].reshape(db, dec, N_HEADS, HEAD_DIM))
        outs["vs"].append(v_all[tp:].reshape(db, dec, N_HEADS, HEAD_DIM))
        outs["lfs"].append(lf_new)
        outs["hs"].append(h_s)
        outs["cs"].append(jnp.swapaxes(cbuf_s_t, 0, 1))

    y_prompt = x[:tp].reshape(batch, seq, d)
    y_sample = x[tp:].reshape(db, dec, d)
    st = {k: jnp.stack(v) for k, v in outs.items()}
    return (y_prompt, y_sample, st["kp"], st["vp"], st["lfp"], st["hp"], st["cp"],
            st["ks"], st["vs"], st["lfs"], st["hs"], st["cs"])
```
